```python
import jax, jax.numpy as jnp
from jax import lax
import numpy as np

D_MODEL = 1024
BATCH = 4
SEQ = 4096
DEPTH = 2

N_MIXERS = 2
N_ATTN_LAYERS = (DEPTH + 1) // 2
N_HGRN_LAYERS = DEPTH // 2
DILATED_GROUPS = ((128, 1), (512, 4), (2048, 16))
N_GROUPS = len(DILATED_GROUPS)
ATTN_HEADS = 8
ATTN_HEAD_DIM = D_MODEL // ATTN_HEADS
ATTN_WIDTH = ATTN_HEADS * ATTN_HEAD_DIM
NUM_BUCKETS = 32
MAX_DISTANCE = 2048
HGRN_HEADS = 8
HGRN_KDIM = 128
HGRN_VDIM = D_MODEL // HGRN_HEADS
HGRN_FWD = HGRN_HEADS * HGRN_KDIM
HGRN_CHUNK = 64
FFN_HIDDEN = ((8 * D_MODEL // 3 + 255) // 256) * 256
RMS_EPS = 1e-6

kernel_name = "hybrid_dilated_attn_hgrn2_swiglu_adaln"


def rms_norm(x, gain, eps=RMS_EPS):
    xf = x.astype(jnp.float32)
    y = xf * lax.rsqrt(jnp.mean(xf * xf, axis=-1, keepdims=True) + eps)
    return (y * gain.astype(jnp.float32)).astype(x.dtype)


def t5_bucket(dist):
    n = np.asarray(dist, dtype=np.int64)
    max_exact = NUM_BUCKETS // 2
    large = max_exact + (np.log(np.maximum(n, 1) / max_exact)
                         / np.log(MAX_DISTANCE / max_exact)
                         * (NUM_BUCKETS - max_exact)).astype(np.int64)
    large = np.minimum(large, NUM_BUCKETS - 1)
    return np.where(n < max_exact, n, large).astype(np.int32)


def dilated_window_group(q, k, v, bias_j, window, dilation):
    B, S, H, Dh = q.shape
    d = dilation
    blk = window // dilation
    span = d * blk
    s_pad = -(-S // span) * span
    pad = s_pad - S
    if pad:
        padw = ((0, 0), (0, pad), (0, 0), (0, 0))
        q, k, v = jnp.pad(q, padw), jnp.pad(k, padw), jnp.pad(v, padw)
    L = s_pad // d
    nb = L // blk

    def to_sub(t):
        X = t.shape[-1]
        return t.reshape(B, L, d, H, X).transpose(0, 2, 1, 3, 4).reshape(B, d, nb, blk, H, X)

    def from_sub(t):
        X = t.shape[-1]
        return t.reshape(B, d, L, H, X).transpose(0, 2, 1, 3, 4).reshape(B, s_pad, H, X)[:, :S]

    qb, kb, vb = to_sub(q), to_sub(k), to_sub(v)
    def with_prev(t):
        prev = jnp.concatenate([jnp.zeros_like(t[:, :, :1]), t[:, :, :-1]], axis=2)
        return jnp.concatenate([prev, t], axis=3)
    kk, vv = with_prev(kb), with_prev(vb)

    qi = np.arange(blk)[:, None]
    ki = np.arange(2 * blk)[None, :]
    j = blk + qi - ki
    band = (j >= 0) & (j <= blk)
    mask = np.broadcast_to(band, (nb, blk, 2 * blk)).copy()
    mask[0, :, :blk] = False
    mask = jnp.asarray(mask.reshape(1, 1, nb, blk, 1, 2 * blk))
    bias = bias_j[np.clip(j, 0, blk)].transpose(0, 2, 1)

    s = jnp.einsum('brnqhd,brnkhd->brnqhk', qb, kk) * (Dh ** -0.5) + bias
    s = jnp.where(mask, s, -jnp.inf)
    m = jnp.max(s, axis=-1, keepdims=True)
    p = jnp.exp(s - m)
    l = jnp.sum(p, axis=-1, keepdims=True)
    o = jnp.einsum('brnqhk,brnkhd->brnqhd', p, vv) / l
    return from_sub(o), from_sub(m), from_sub(l)


def dilated_attention(h, w_qkv, w_out, q_gain, k_gain, rel_bias):
    B, S, _ = h.shape
    qkv = (h @ w_qkv).astype(jnp.float32).reshape(B, S, N_GROUPS, 3, ATTN_HEADS, ATTN_HEAD_DIM)
    outs, maxes, dens = [], [], []
    for g, (window, dil) in enumerate(DILATED_GROUPS):
        q = rms_norm(qkv[:, :, g, 0], q_gain[g])
        k = rms_norm(qkv[:, :, g, 1], k_gain[g])
        v = qkv[:, :, g, 2]
        steps = window // dil
        bucket = t5_bucket(np.arange(steps + 1) * dil)
        bias_j = rel_bias[bucket, g * ATTN_HEADS:(g + 1) * ATTN_HEADS].astype(jnp.float32)
        o, m, l = dilated_window_group(q, k, v, bias_j, window, dil)
        outs.append(o); maxes.append(m); dens.append(l)
    o = jnp.stack(outs)
    m = jnp.stack(maxes)
    l = jnp.stack(dens)
    wts = l * jnp.exp(m - jnp.max(m, axis=0, keepdims=True))
    out = jnp.sum(wts * o, axis=0) / jnp.sum(wts, axis=0)
    return out.reshape(B, S, ATTN_WIDTH).astype(h.dtype) @ w_out


def hgrn2_mixer(h, w_in, w_out, gnorm_gain, lower_bound):
    B, S, _ = h.shape
    proj = h @ w_in
    q, f, i, g = jnp.split(proj, [HGRN_FWD, 2 * HGRN_FWD, 2 * HGRN_FWD + D_MODEL], axis=-1)
    q = jax.nn.silu(q.astype(jnp.float32))
    lb = lower_bound.astype(jnp.float32)
    fg = lb + (1.0 - lb) * jax.nn.sigmoid(f.astype(jnp.float32))
    logf = jnp.log(fg)
    kin = 1.0 - fg
    nc = S // HGRN_CHUNK

    def chunk(t, dim):
        t = t.reshape(B, S, HGRN_HEADS, dim).transpose(0, 2, 1, 3)
        return jnp.moveaxis(t.reshape(B, HGRN_HEADS, nc, HGRN_CHUNK, dim), 2, 0)

    qc, kc, gc = chunk(q, HGRN_KDIM), chunk(kin, HGRN_KDIM), chunk(logf, HGRN_KDIM)
    vc = chunk(i.astype(jnp.float32), HGRN_VDIM)
    tri = jnp.asarray(np.tril(np.ones((HGRN_CHUNK, HGRN_CHUNK), dtype=bool))[None, None, :, :, None])

    def step(state, xs):
        qq, kq, vq, gq = xs
        G = jnp.cumsum(gq, axis=2)
        diff = G[:, :, :, None, :] - G[:, :, None, :, :]
        decay = jnp.exp(jnp.where(tri, diff, -jnp.inf))
        A = jnp.einsum('bhtk,bhsk,bhtsk->bhts', qq, kq, decay)
        o = jnp.einsum('bhts,bhsv->bhtv', A, vq) + jnp.einsum('bhtk,bhkv->bhtv', qq * jnp.exp(G), state)
        G_last = G[:, :, -1:, :]
        new_state = (jnp.exp(G_last[:, :, 0, :])[..., None] * state
                     + jnp.einsum('bhsk,bhsv->bhkv', kq * jnp.exp(G_last - G), vq))
        return new_state, o

    state0 = jnp.zeros((B, HGRN_HEADS, HGRN_KDIM, HGRN_VDIM), jnp.float32)
    _, o = lax.scan(step, state0, (qc, kc, vc, gc))
    o = jnp.moveaxis(o, 0, 2).reshape(B, HGRN_HEADS, S, HGRN_VDIM).transpose(0, 2, 1, 3)
    o = rms_norm(o, gnorm_gain).reshape(B, S, D_MODEL)
    o = o * jax.nn.silu(g.astype(jnp.float32))
    return o.astype(h.dtype) @ w_out


def swiglu(h, w1, w3, w2):
    return (jax.nn.silu(h @ w1) * (h @ w3)) @ w2


def setup_inputs(seed: int = 0) -> dict:
    key = jax.random.key(seed)
    ks = jax.random.split(key, 20)
    f32 = jnp.float32
    def nrm(k, shape, fan_in, mult=1.0):
        return jax.random.normal(k, shape, f32) * (mult * fan_in ** -0.5)
    def gain(k, shape):
        return 1.0 + 0.05 * jax.random.normal(k, shape, f32)
    return {
        "x": jax.random.normal(ks[0], (BATCH, SEQ, D_MODEL), f32),
        "c": jax.random.normal(ks[1], (BATCH, D_MODEL), f32),
        "rel_bias": 0.5 * jax.random.normal(ks[2], (NUM_BUCKETS, N_GROUPS * ATTN_HEADS), f32),
        "ada_w": nrm(ks[3], (DEPTH, D_MODEL, 6 * D_MODEL), D_MODEL, 0.5),
        "ada_b": 0.02 * jax.random.normal(ks[4], (DEPTH, 6 * D_MODEL), f32),
        "norm_mix": gain(ks[5], (DEPTH, D_MODEL)),
        "norm_ffn": gain(ks[6], (DEPTH, D_MODEL)),
        "attn_w_qkv": nrm(ks[7], (N_ATTN_LAYERS, D_MODEL, N_GROUPS * 3 * ATTN_WIDTH), D_MODEL),
        "attn_w_out": nrm(ks[8], (N_ATTN_LAYERS, ATTN_WIDTH, D_MODEL), ATTN_WIDTH),
        "attn_q_gain": gain(ks[9], (N_ATTN_LAYERS, N_GROUPS, ATTN_HEAD_DIM)),
        "attn_k_gain": gain(ks[10], (N_ATTN_LAYERS, N_GROUPS, ATTN_HEAD_DIM)),
        "hgrn_w_in": nrm(ks[11], (N_HGRN_LAYERS, D_MODEL, 2 * HGRN_FWD + 2 * D_MODEL), D_MODEL),
        "hgrn_w_out": nrm(ks[12], (N_HGRN_LAYERS, D_MODEL, D_MODEL), D_MODEL),
        "hgrn_gnorm": gain(ks[13], (N_HGRN_LAYERS, HGRN_VDIM)),
        "hgrn_lower_bounds": jax.random.normal(ks[14], (DEPTH, HGRN_FWD), f32),
        "ffn_w1": nrm(ks[15], (DEPTH, D_MODEL, FFN_HIDDEN), D_MODEL),
        "ffn_w3": nrm(ks[16], (DEPTH, D_MODEL, FFN_HIDDEN), D_MODEL),
        "ffn_w2": nrm(ks[17], (DEPTH, FFN_HIDDEN, D_MODEL), FFN_HIDDEN),
    }


def reference(x, c, rel_bias, ada_w, ada_b, norm_mix, norm_ffn, attn_w_qkv, attn_w_out,
              attn_q_gain, attn_k_gain, hgrn_w_in, hgrn_w_out, hgrn_gnorm, hgrn_lower_bounds,
              ffn_w1, ffn_w3, ffn_w2):
    p = jax.nn.softmax(hgrn_lower_bounds.astype(jnp.float32), axis=0)
    lbs = jnp.cumsum(p, axis=0) - p[0]
    c_act = jax.nn.silu(c)
    for i in range(DEPTH):
        mod = c_act @ ada_w[i] + ada_b[i]
        sh1, sc1, g1, sh2, sc2, g2 = jnp.split(mod[:, None, :], 6, axis=-1)
        h = rms_norm(x, norm_mix[i]) * (1.0 + sc1) + sh1
        a = i // N_MIXERS
        if i % N_MIXERS == 0:
            y = dilated_attention(h, attn_w_qkv[a], attn_w_out[a], attn_q_gain[a],
                                  attn_k_gain[a], rel_bias)
        else:
            y = hgrn2_mixer(h, hgrn_w_in[a], hgrn_w_out[a], hgrn_gnorm[a], lbs[i])
        x = x + g1 * y
        h = rms_norm(x, norm_ffn[i]) * (1.0 + sc2) + sh2
        x = x + g2 * swiglu(h, ffn_w1[i], ffn_w3[i], ffn_w2[i])
    return x
```

```python
import functools

import jax
import jax.numpy as jnp
import numpy as np
from jax import lax
from jax.experimental import pallas as pl
from jax.experimental.pallas import tpu as pltpu

D_MODEL = 1024
DEPTH = 2
DILATED_GROUPS = ((128, 1), (512, 4), (2048, 16))
N_GROUPS = len(DILATED_GROUPS)
HEADS = 8
HEAD_DIM = D_MODEL // HEADS
ATTN_BLOCK = 128
NUM_BUCKETS = 32
MAX_DISTANCE = 2048
HGRN_CHUNK = 64
FFN_HIDDEN = ((8 * D_MODEL // 3 + 255) // 256) * 256
RMS_EPS = 1e-6

VMEM_LIMIT_BYTES = 56 * 1024 * 1024
LSE_LANES = HEAD_DIM // HEADS

F32 = jnp.float32
BF16 = jnp.bfloat16
NEG_INF = float("-inf")


def _params(*semantics):
    return pltpu.CompilerParams(dimension_semantics=semantics,
                                vmem_limit_bytes=VMEM_LIMIT_BYTES)


def _modulated_norm(x, gain, shift, scale):
    ms = jnp.mean(x * x, axis=-1, keepdims=True)
    y = x * lax.rsqrt(ms + RMS_EPS)
    return (y * gain) * (1.0 + scale) + shift


def _mods_kernel(c_ref, w_ref, b_ref, o_ref):
    c = c_ref[...]
    c_act = c * jax.nn.sigmoid(c)
    o_ref[0] = jnp.dot(c_act, w_ref[0], preferred_element_type=F32,
                       precision=lax.Precision.HIGHEST) + b_ref[0]


def _mods_call(c_pad, ada_w, ada_b):
    rows = c_pad.shape[0]
    n = ada_w.shape[-1]
    tn = 2048
    return pl.pallas_call(
        _mods_kernel,
        out_shape=jax.ShapeDtypeStruct((DEPTH, rows, n), F32),
        grid=(DEPTH, n // tn),
        in_specs=[
            pl.BlockSpec((rows, D_MODEL), lambda l, j: (0, 0)),
            pl.BlockSpec((1, D_MODEL, tn), lambda l, j: (l, 0, j)),
            pl.BlockSpec((1, 1, tn), lambda l, j: (l, 0, j)),
        ],
        out_specs=pl.BlockSpec((1, rows, tn), lambda l, j: (l, 0, j)),
        compiler_params=_params("arbitrary", "arbitrary"),
        name="adaln_mods",
    )(c_pad, ada_w, ada_b.reshape(DEPTH, 1, n))


def _t5_bucket(dist):
    n = np.asarray(dist, dtype=np.int64)
    max_exact = NUM_BUCKETS // 2
    large = max_exact + (np.log(np.maximum(n, 1) / max_exact)
                         / np.log(MAX_DISTANCE / max_exact)
                         * (NUM_BUCKETS - max_exact)).astype(np.int64)
    large = np.minimum(large, NUM_BUCKETS - 1)
    return np.where(n < max_exact, n, large).astype(np.int32)


def _bucket_tiles():
    blk = ATTN_BLOCK
    qi = np.arange(blk)[:, None]
    ki = np.arange(2 * blk)[None, :]
    j = blk + qi - ki
    band = (j >= 0) & (j <= blk)
    tiles = []
    for _, dil in DILATED_GROUPS:
        bucket = _t5_bucket(np.clip(j, 0, blk) * dil)
        tiles.append(np.where(band, bucket, -1).astype(np.int32))
    return np.stack(tiles)


def _bias_kernel(rb_ref, idx_ref, o_ref):
    col = pl.program_id(0) * HEADS + pl.program_id(1)
    idx = idx_ref[0]
    tile = jnp.full(idx.shape, NEG_INF, F32)
    for b in range(NUM_BUCKETS):
        tile = jnp.where(idx == b, rb_ref[b, col], tile)
    o_ref[0, 0] = tile


def _bias_call(rel_bias):
    idx = jnp.asarray(_bucket_tiles())
    blk = ATTN_BLOCK
    return pl.pallas_call(
        _bias_kernel,
        out_shape=jax.ShapeDtypeStruct((N_GROUPS, HEADS, blk, 2 * blk), F32),
        grid=(N_GROUPS, HEADS),
        in_specs=[
            pl.BlockSpec(memory_space=pltpu.SMEM),
            pl.BlockSpec((1, blk, 2 * blk), lambda g, h: (g, 0, 0)),
        ],
        out_specs=pl.BlockSpec((1, 1, blk, 2 * blk), lambda g, h: (g, h, 0, 0)),
        compiler_params=_params("arbitrary", "arbitrary"),
        name="rel_bias_tiles",
    )(rel_bias, idx)


def _qkv_kernel(x_ref, mod_ref, gain_ref, w_ref, hg_ref, o_ref, h_scr):
    j = pl.program_id(1)

    @pl.when(j == 0)
    def _():
        mod = mod_ref[0, 0]
        h = _modulated_norm(x_ref[...], gain_ref[...], mod[0:1], mod[1:2])
        h_scr[...] = h.astype(BF16)

    acc = jnp.dot(h_scr[...], w_ref[...], preferred_element_type=F32)
    part = j % 3

    @pl.when(part == 2)
    def _():
        o_ref[...] = acc.astype(BF16)

    @pl.when(part < 2)
    def _():
        g = hg_ref[0]
        for h in range(HEADS):
            sl = slice(h * HEAD_DIM, (h + 1) * HEAD_DIM)
            a = acc[:, sl]
            ms = jnp.mean(a * a, axis=-1, keepdims=True)
            o_ref[:, sl] = (a * lax.rsqrt(ms + RMS_EPS) * g).astype(BF16)


def _qkv_call(x2, mods, layer, gain, w, head_gains, seq):
    m = x2.shape[0]
    n = w.shape[1]
    tm, tn = 1024, D_MODEL
    tiles_per_batch = seq // tm
    return pl.pallas_call(
        _qkv_kernel,
        out_shape=jax.ShapeDtypeStruct((m, n), BF16),
        grid=(m // tm, n // tn),
        in_specs=[
            pl.BlockSpec((tm, D_MODEL), lambda i, j: (i, 0)),
            pl.BlockSpec((1, 1, 6, D_MODEL), lambda i, j: (layer, i // tiles_per_batch, 0, 0)),
            pl.BlockSpec((1, D_MODEL), lambda i, j: (0, 0)),
            pl.BlockSpec((D_MODEL, tn), lambda i, j: (0, j)),
            pl.BlockSpec((1, 1, HEAD_DIM), lambda i, j: (j, 0, 0)),
        ],
        out_specs=pl.BlockSpec((tm, tn), lambda i, j: (i, j)),
        scratch_shapes=[pltpu.VMEM((tm, D_MODEL), BF16)],
        compiler_params=_params("arbitrary", "arbitrary"),
        name="attn_qkv_proj",
    )(x2, mods, gain, w, head_gains)


def _attn_kernel(q_ref, k_ref, v_ref, bias_ref, o_ref, lse_ref, kprev, vprev):
    first = pl.program_id(2) == 0

    @pl.when(first)
    def _():
        kprev[...] = jnp.zeros_like(kprev)
        vprev[...] = jnp.zeros_like(vprev)

    blk = ATTN_BLOCK
    nt = (((1,), (1,)), ((), ()))
    lane = lax.broadcasted_iota(jnp.int32, (blk, HEAD_DIM), 1)
    lse_tile = jnp.zeros((blk, HEAD_DIM), F32)
    for h in range(HEADS):
        sl = slice(h * HEAD_DIM, (h + 1) * HEAD_DIM)
        q = q_ref[0, :, sl]
        s_cur = lax.dot_general(q, k_ref[0, :, sl], nt, preferred_element_type=F32)
        s_cur = s_cur + bias_ref[h, :, blk:]
        s_prev = lax.dot_general(q, kprev[:, sl], nt, preferred_element_type=F32)
        s_prev = s_prev + jnp.where(first, NEG_INF, bias_ref[h, :, :blk])
        m = jnp.maximum(jnp.max(s_cur, axis=-1, keepdims=True),
                        jnp.max(s_prev, axis=-1, keepdims=True))
        p_cur = jnp.exp(s_cur - m)
        p_prev = jnp.exp(s_prev - m)
        l = jnp.sum(p_cur, axis=-1, keepdims=True) + jnp.sum(p_prev, axis=-1, keepdims=True)
        o = jnp.dot(p_cur.astype(BF16), v_ref[0, :, sl], preferred_element_type=F32)
        o = o + jnp.dot(p_prev.astype(BF16), vprev[:, sl], preferred_element_type=F32)
        o_ref[0, :, sl] = (o / l).astype(BF16)
        lse = m + jnp.log(l)
        lse_tile = jnp.where((lane >= h * LSE_LANES) & (lane < (h + 1) * LSE_LANES), lse, lse_tile)
    lse_ref[0] = lse_tile
    kprev[...] = k_ref[0]
    vprev[...] = v_ref[0]


def _attn_call(qkv, bias_g, g, dil, batch, seq):
    blk = ATTN_BLOCK
    n_cols = qkv.shape[-1]
    cb = n_cols // D_MODEL
    rows = seq // dil
    view = qkv.reshape(batch, rows, dil * n_cols)
    nb = rows // blk
    qspec = lambda part: pl.BlockSpec(
        (1, blk, D_MODEL), lambda b, r, n: (b, n, r * cb + g * 3 + part))
    o, lse = pl.pallas_call(
        _attn_kernel,
        out_shape=(jax.ShapeDtypeStruct((batch, rows, dil * D_MODEL), BF16),
                   jax.ShapeDtypeStruct((batch, rows, dil * HEAD_DIM), F32)),
        grid=(batch, dil, nb),
        in_specs=[qspec(0), qspec(1), qspec(2),
                  pl.BlockSpec((HEADS, blk, 2 * blk), lambda b, r, n: (0, 0, 0))],
        out_specs=(pl.BlockSpec((1, blk, D_MODEL), lambda b, r, n: (b, n, r)),
                   pl.BlockSpec((1, blk, HEAD_DIM), lambda b, r, n: (b, n, r))),
        scratch_shapes=[pltpu.VMEM((blk, D_MODEL), BF16), pltpu.VMEM((blk, D_MODEL), BF16)],
        compiler_params=_params("arbitrary", "arbitrary", "arbitrary"),
        name=f"dilated_attn_g{g}",
    )(view, view, view, bias_g)
    return o.reshape(batch * seq, D_MODEL), lse.reshape(batch * seq, HEAD_DIM)


def _outproj_kernel(*refs, n_src):
    o_refs = refs[:n_src]
    lse_refs = refs[n_src:2 * n_src] if n_src > 1 else ()
    w_ref, x_ref, mod_ref, out_ref = refs[-5], refs[-4], refs[-3], refs[-2]
    a_scr = refs[-1]
    if n_src == 1:
        a = o_refs[0][...]
    else:
        ls = [r[...] for r in lse_refs]
        mx = functools.reduce(jnp.maximum, ls)
        ws = [jnp.exp(l - mx) for l in ls]
        inv = 1.0 / functools.reduce(lambda p, q: p + q, ws)
        wn = [w * inv for w in ws]
        for h in range(HEADS):
            sl = slice(h * HEAD_DIM, (h + 1) * HEAD_DIM)
            acc = None
            for g in range(n_src):
                wcol = wn[g][:, h * LSE_LANES:h * LSE_LANES + 1]
                term = wcol * o_refs[g][:, sl].astype(F32)
                acc = term if acc is None else acc + term
            a_scr[:, sl] = acc.astype(BF16)
        a = a_scr[...]
    y = jnp.dot(a, w_ref[...], preferred_element_type=F32)
    gate = mod_ref[0, 0][2:3]
    out_ref[...] = x_ref[...] + gate * y


def _outproj_call(srcs, lses, w, x2, mods, layer, seq):
    m = x2.shape[0]
    tm = 512
    tiles_per_batch = seq // tm
    n_src = len(srcs)
    row = lambda i: (i, 0)
    in_specs = [pl.BlockSpec((tm, D_MODEL), row) for _ in srcs]
    in_specs += [pl.BlockSpec((tm, HEAD_DIM), row) for _ in lses]
    in_specs += [
        pl.BlockSpec((D_MODEL, D_MODEL), lambda i: (0, 0)),
        pl.BlockSpec((tm, D_MODEL), row),
        pl.BlockSpec((1, 1, 6, D_MODEL), lambda i: (layer, i // tiles_per_batch, 0, 0)),
    ]
    return pl.pallas_call(
        functools.partial(_outproj_kernel, n_src=n_src),
        out_shape=jax.ShapeDtypeStruct((m, D_MODEL), F32),
        grid=(m // tm,),
        in_specs=in_specs,
        out_specs=pl.BlockSpec((tm, D_MODEL), row),
        scratch_shapes=[pltpu.VMEM((tm, D_MODEL), BF16)],
        compiler_params=_params("arbitrary"),
        name=f"mixer_out_proj_l{layer}",
    )(*srcs, *lses, w, x2, mods)


FFN_CHUNK = 256


def _ffn_kernel(x_ref, mod_ref, gain_ref, w1_ref, w3_ref, w2_ref, out_ref, h_scr, u_scr):
    x = x_ref[...]
    mod = mod_ref[0, 0]
    h_scr[...] = _modulated_norm(x, gain_ref[...], mod[3:4], mod[4:5]).astype(BF16)
    for c in range(FFN_HIDDEN // FFN_CHUNK):
        sl = slice(c * FFN_CHUNK, (c + 1) * FFN_CHUNK)
        a = jnp.dot(h_scr[...], w1_ref[:, sl], preferred_element_type=F32)
        b = jnp.dot(h_scr[...], w3_ref[:, sl], preferred_element_type=F32)
        u_scr[:, sl] = (a * jax.nn.sigmoid(a) * b).astype(BF16)
    y = jnp.dot(u_scr[...], w2_ref[...], preferred_element_type=F32)
    out_ref[...] = x + mod[5:6] * y


def _ffn_call(x2, mods, layer, gain, w1, w3, w2, seq):
    m = x2.shape[0]
    tm = 512
    tiles_per_batch = seq // tm
    const = lambda i: (0, 0)
    return pl.pallas_call(
        _ffn_kernel,
        out_shape=jax.ShapeDtypeStruct((m, D_MODEL), F32),
        grid=(m // tm,),
        in_specs=[
            pl.BlockSpec((tm, D_MODEL), lambda i: (i, 0)),
            pl.BlockSpec((1, 1, 6, D_MODEL), lambda i: (layer, i // tiles_per_batch, 0, 0)),
            pl.BlockSpec((1, D_MODEL), const),
            pl.BlockSpec((D_MODEL, FFN_HIDDEN), const, pipeline_mode=pl.Buffered(1)),
            pl.BlockSpec((D_MODEL, FFN_HIDDEN), const, pipeline_mode=pl.Buffered(1)),
            pl.BlockSpec((FFN_HIDDEN, D_MODEL), const, pipeline_mode=pl.Buffered(1)),
        ],
        out_specs=pl.BlockSpec((tm, D_MODEL), lambda i: (i, 0)),
        scratch_shapes=[pltpu.VMEM((tm, D_MODEL), BF16), pltpu.VMEM((tm, FFN_HIDDEN), BF16)],
        compiler_params=_params("arbitrary"),
        name=f"swiglu_ffn_l{layer}",
    )(x2, mods, gain, w1, w3, w2)


def _hgrn_in_kernel(x_ref, mod_ref, gain_ref, w_ref, lbp_ref, q_ref, fg_ref, v_ref, sg_ref,
                    h_scr, *, layer):
    j = pl.program_id(1)

    @pl.when(j == 0)
    def _():
        mod = mod_ref[0, 0]
        h = _modulated_norm(x_ref[...], gain_ref[...], mod[0:1], mod[1:2])
        h_scr[...] = h.astype(BF16)

    acc = jnp.dot(h_scr[...], w_ref[...], preferred_element_type=F32)

    @pl.when(j == 0)
    def _():
        q_ref[...] = (acc * jax.nn.sigmoid(acc)).astype(BF16)

    @pl.when(j == 1)
    def _():
        rows = [lbp_ref[r:r + 1, :] for r in range(DEPTH)]
        mx = functools.reduce(jnp.maximum, rows)
        es = [jnp.exp(r - mx) for r in rows]
        den = functools.reduce(lambda a, b: a + b, es)
        lb = jnp.zeros_like(den)
        for r in range(1, layer + 1):
            lb = lb + es[r] / den
        fg_ref[...] = lb + (1.0 - lb) * jax.nn.sigmoid(acc)

    @pl.when(j == 2)
    def _():
        v_ref[...] = acc.astype(BF16)

    @pl.when(j == 3)
    def _():
        sg_ref[...] = (acc * jax.nn.sigmoid(acc)).astype(BF16)


def _hgrn_in_call(x2, mods, layer, gain, w, lower_bounds, seq):
    m = x2.shape[0]
    tm, tn = 1024, D_MODEL
    tiles_per_batch = seq // tm
    row = lambda i, j: (i, 0)
    bf = jax.ShapeDtypeStruct((m, D_MODEL), BF16)
    return pl.pallas_call(
        functools.partial(_hgrn_in_kernel, layer=layer),
        out_shape=(bf, jax.ShapeDtypeStruct((m, D_MODEL), F32), bf, bf),
        grid=(m // tm, w.shape[1] // tn),
        in_specs=[
            pl.BlockSpec((tm, D_MODEL), row),
            pl.BlockSpec((1, 1, 6, D_MODEL), lambda i, j: (layer, i // tiles_per_batch, 0, 0)),
            pl.BlockSpec((1, D_MODEL), lambda i, j: (0, 0)),
            pl.BlockSpec((D_MODEL, tn), lambda i, j: (0, j)),
            pl.BlockSpec((DEPTH, D_MODEL), lambda i, j: (0, 0)),
        ],
        out_specs=tuple(pl.BlockSpec((tm, D_MODEL), row) for _ in range(4)),
        scratch_shapes=[pltpu.VMEM((tm, D_MODEL), BF16)],
        compiler_params=_params("arbitrary", "arbitrary"),
        name="hgrn_in_proj",
    )(x2, mods, gain, w, lower_bounds)


HGRN_LEVELS = (32, 16, 8)
HGRN_BLOCK = 8
HGRN_ROWS = 512


def _split3(x):
    hi = x.astype(BF16)
    r1 = x - hi.astype(F32)
    mid = r1.astype(BF16)
    lo = (r1 - mid.astype(F32)).astype(BF16)
    return hi, mid, lo


def _hgrn_chunk_head(q, kin, v, g, st, mask_ref):
    c = HGRN_CHUNK
    nt = (((1,), (1,)), ((), ()))
    g_last = g[c - 1:c, :]
    qg = (q * jnp.exp(g)).astype(BF16)
    o = lax.dot_general(qg, st.astype(BF16), nt, preferred_element_type=F32)
    a_off = None
    for li, b in enumerate(HGRN_LEVELS):
        q_parts, k_parts = [], []
        for p0 in range(0, c, 2 * b):
            g_mid = g[p0 + b - 1:p0 + b, :]
            k_parts.append(kin[p0:p0 + b] * jnp.exp(g_mid - g[p0:p0 + b]))
            k_parts.append(jnp.zeros((b, HEAD_DIM), F32))
            q_parts.append(jnp.zeros((b, HEAD_DIM), F32))
            q_parts.append(q[p0 + b:p0 + 2 * b] * jnp.exp(g[p0 + b:p0 + 2 * b] - g_mid))
        q_l = jnp.concatenate(q_parts, axis=0).astype(BF16)
        k_l = jnp.concatenate(k_parts, axis=0).astype(BF16)
        p_l = lax.dot_general(q_l, k_l, nt, preferred_element_type=F32) * mask_ref[li]
        a_off = p_l if a_off is None else a_off + p_l
    o = o + jnp.dot(a_off.astype(BF16), v, preferred_element_type=F32)
    vf = v.astype(F32)
    row = lax.broadcasted_iota(jnp.int32, (HGRN_BLOCK, HEAD_DIM), 0)
    diag = []
    for r0 in range(0, c, HGRN_BLOCK):
        gb = g[r0:r0 + HGRN_BLOCK]
        qb = q[r0:r0 + HGRN_BLOCK]
        kb = kin[r0:r0 + HGRN_BLOCK]
        vb = vf[r0:r0 + HGRN_BLOCK]
        ob = jnp.zeros((HGRN_BLOCK, HEAD_DIM), F32)
        for s in range(HGRN_BLOCK):
            e = jnp.exp(jnp.where(row >= s, gb - gb[s:s + 1, :], NEG_INF))
            a = jnp.sum(qb * (kb[s:s + 1, :] * e), axis=-1, keepdims=True)
            ob = ob + a * vb[s:s + 1, :]
        diag.append(ob)
    o = o + jnp.concatenate(diag, axis=0)
    k_end = (kin * jnp.exp(g_last - g)).astype(BF16)
    upd = lax.dot_general(v, k_end, (((0,), (0,)), ((), ())), preferred_element_type=F32)
    st_new = st * jnp.exp(g_last) + upd
    return o, st_new


def _hgrn_kernel(q_ref, fg_ref, v_ref, sg_ref, gn_ref, tri_ref, mask_ref, y_ref, st_scr, g_scr):
    @pl.when(pl.program_id(1) == 0)
    def _():
        st_scr[...] = jnp.zeros_like(st_scr)

    c = HGRN_CHUNK
    gn = gn_ref[...]

    def chunk(ci, carry):
        r0 = pl.multiple_of(ci * c, c)
        rows = pl.ds(r0, c)
        logf = jnp.log(fg_ref[0, rows, :])
        tri = tri_ref[...]
        g_all = None
        for part in _split3(logf):
            t = jnp.dot(tri, part, preferred_element_type=F32)
            g_all = t if g_all is None else g_all + t
        g_scr[...] = g_all
        for h in range(HEADS):
            sl = slice(h * HEAD_DIM, (h + 1) * HEAD_DIM)
            q = q_ref[0, rows, sl].astype(F32)
            kin = 1.0 - fg_ref[0, rows, sl]
            v = v_ref[0, rows, sl]
            o, st_new = _hgrn_chunk_head(q, kin, v, g_scr[:, sl], st_scr[h], mask_ref)
            st_scr[h] = st_new
            ms = jnp.mean(o * o, axis=-1, keepdims=True)
            y = (o * lax.rsqrt(ms + RMS_EPS) * gn) * sg_ref[0, rows, sl].astype(F32)
            y_ref[0, rows, sl] = y.astype(BF16)
        return carry

    lax.fori_loop(0, HGRN_ROWS // c, chunk, 0)


def _hgrn_level_masks():
    c = HGRN_CHUNK
    t = np.arange(c)[:, None]
    s = np.arange(c)[None, :]
    masks = []
    for b in HGRN_LEVELS:
        same = (t // (2 * b)) == (s // (2 * b))
        masks.append((same & (t % (2 * b) >= b) & (s % (2 * b) < b)).astype(np.float32))
    return np.stack(masks)


def _hgrn_call(q, fg, v, sg, gnorm, batch, seq):
    c = HGRN_CHUNK
    t = HGRN_ROWS
    shp = (batch, seq, D_MODEL)
    tri = jnp.asarray(np.tril(np.ones((c, c), np.float32)), BF16)
    masks = jnp.asarray(_hgrn_level_masks())
    blk = lambda: pl.BlockSpec((1, t, D_MODEL), lambda b, s: (b, s, 0))
    y = pl.pallas_call(
        _hgrn_kernel,
        out_shape=jax.ShapeDtypeStruct(shp, BF16),
        grid=(batch, seq // t),
        in_specs=[blk(), blk(), blk(), blk(),
                  pl.BlockSpec((1, HEAD_DIM), lambda b, s: (0, 0)),
                  pl.BlockSpec((c, c), lambda b, s: (0, 0)),
                  pl.BlockSpec((len(HGRN_LEVELS), c, c), lambda b, s: (0, 0, 0))],
        out_specs=blk(),
        scratch_shapes=[pltpu.VMEM((HEADS, HEAD_DIM, HEAD_DIM), F32),
                        pltpu.VMEM((c, D_MODEL), F32)],
        compiler_params=_params("arbitrary", "arbitrary"),
        name="hgrn2_recurrence",
    )(q.reshape(shp), fg.reshape(shp), v.reshape(shp), sg.reshape(shp),
      gnorm.reshape(1, HEAD_DIM), tri, masks)
    return y.reshape(batch * seq, D_MODEL)


def kernel(x, c, rel_bias, ada_w, ada_b, norm_mix, norm_ffn, attn_w_qkv, attn_w_out,
           attn_q_gain, attn_k_gain, hgrn_w_in, hgrn_w_out, hgrn_gnorm, hgrn_lower_bounds,
           ffn_w1, ffn_w3, ffn_w2):
    batch, seq, _ = x.shape
    x2 = x.reshape(batch * seq, D_MODEL)

    pad_rows = 8
    c_pad = jnp.zeros((pad_rows, D_MODEL), F32).at[:batch].set(c)
    mods = _mods_call(c_pad, ada_w, ada_b)[:, :batch].reshape(DEPTH, batch, 6, D_MODEL)

    bias = _bias_call(rel_bias)

    for layer in range(DEPTH):
        a = layer // 2
        gain_mix = norm_mix[layer].reshape(1, D_MODEL)
        if layer % 2 == 0:
            ones = jnp.ones((HEAD_DIM,), F32)
            head_gains = jnp.stack(
                [t for g in range(N_GROUPS)
                 for t in (attn_q_gain[a, g] * (HEAD_DIM ** -0.5), attn_k_gain[a, g], ones)]
            ).reshape(N_GROUPS * 3, 1, HEAD_DIM)
            qkv = _qkv_call(x2, mods, layer, gain_mix, attn_w_qkv[a].astype(BF16), head_gains, seq)
            outs, lses = [], []
            for g, (_, dil) in enumerate(DILATED_GROUPS):
                o, lse = _attn_call(qkv, bias[g], g, dil, batch, seq)
                outs.append(o)
                lses.append(lse)
            x2 = _outproj_call(outs, lses, attn_w_out[a].astype(BF16), x2, mods, layer, seq)
        else:
            q, fg, v, sg = _hgrn_in_call(x2, mods, layer, gain_mix, hgrn_w_in[a].astype(BF16),
                                         hgrn_lower_bounds, seq)
            y = _hgrn_call(q, fg, v, sg, hgrn_gnorm[a], batch, seq)
            x2 = _outproj_call([y], [], hgrn_w_out[a].astype(BF16), x2, mods, layer, seq)
        x2 = _ffn_call(x2, mods, layer, norm_ffn[layer].reshape(1, D_MODEL),
                       ffn_w1[layer].astype(BF16), ffn_w3[layer].astype(BF16),
                       ffn_w2[layer].astype(BF16), seq)
    return x2.reshape(batch, seq, D_MODEL)
```
